```python
import math
import jax, jax.numpy as jnp
from jax import lax
import numpy as np

D_MODEL = 2048
BATCH = 4
SEQ = 4096
DEPTH = 4

D_MIX = D_MODEL
ATTN_WIDTH = D_MIX // 2
POOL_WIDTH = D_MIX - ATTN_WIDTH
HEAD_DIM = 64
N_HEADS = ATTN_WIDTH // HEAD_DIM
N_KV_HEADS = N_HEADS // 4
GQA_GROUP = N_HEADS // N_KV_HEADS
WINDOW = 128
BLOCK = 128
POOL_SIZES = (2, 4, 8, 16)
N_POOL_GROUPS = len(POOL_SIZES)
POOL_GROUP_DIM = POOL_WIDTH // N_POOL_GROUPS
D_FF = int(round(8 * D_MODEL / 3 / 128)) * 128
CONV_WIDTH = 3
N_BUCKETS = 32
MAX_DISTANCE = 128
DEEPNORM_ALPHA = (2 * DEPTH) ** 0.25
DEEPNORM_BETA = (8 * DEPTH) ** -0.25
LN_EPS = 1e-5
MASK_VALUE = -1e30
Q_COLS = N_HEADS * HEAD_DIM
KV_COLS = N_KV_HEADS * HEAD_DIM
IN_COLS = Q_COLS + 2 * KV_COLS + POOL_WIDTH

kernel_name = "hymba_style_window_gqa_multiscale_pool_convglu_deepnorm"


def layer_norm(x, g, b):
    xf = x.astype(jnp.float32)
    mu = jnp.mean(xf, axis=-1, keepdims=True)
    var = jnp.mean(jnp.square(xf - mu), axis=-1, keepdims=True)
    return ((xf - mu) * lax.rsqrt(var + LN_EPS) * g.astype(jnp.float32) + b.astype(jnp.float32)).astype(x.dtype)


def t5_bucket(rel):
    half = N_BUCKETS // 2
    max_exact = half // 2
    base = jnp.where(rel > 0, half, 0)
    n = jnp.abs(rel)
    nf = jnp.maximum(n, 1).astype(jnp.float32)
    large = max_exact + (jnp.log(nf / max_exact) / math.log(MAX_DISTANCE / max_exact)
                         * (half - max_exact)).astype(jnp.int32)
    large = jnp.minimum(large, half - 1)
    return base + jnp.where(n < max_exact, n, large)


def banded_bias_and_mask(rel_bias, seq):
    n_blocks = seq // BLOCK
    q_off = jnp.arange(BLOCK)[:, None]
    k_off = jnp.arange(3 * BLOCK)[None, :] - BLOCK
    rel = k_off - q_off
    bias = jnp.transpose(rel_bias[t5_bucket(rel)], (2, 0, 1))
    band = jnp.abs(rel) <= WINDOW
    key_pos = jnp.arange(n_blocks)[:, None] * BLOCK + k_off
    valid = (key_pos >= 0) & (key_pos < seq)
    mask = band[None] & valid[:, None, :]
    return bias, mask


def windowed_gqa(q, k, v, sink, pos_bias, mask):
    b, s = q.shape[0], q.shape[1]
    nb = s // BLOCK
    qb = q.reshape(b, nb, BLOCK, N_KV_HEADS, GQA_GROUP, HEAD_DIM)

    def band(t):
        tp = jnp.pad(t, ((0, 0), (BLOCK, BLOCK), (0, 0), (0, 0)))
        tp = tp.reshape(b, nb + 2, BLOCK, N_KV_HEADS, HEAD_DIM)
        return jnp.concatenate([tp[:, :-2], tp[:, 1:-1], tp[:, 2:]], axis=2)

    kb, vb = band(k), band(v)
    scores = jnp.einsum('bnqkgd,bnskd->bnkgqs', qb, kb).astype(jnp.float32) * (HEAD_DIM ** -0.5)
    scores = scores + pos_bias.reshape(N_KV_HEADS, GQA_GROUP, BLOCK, 3 * BLOCK).astype(jnp.float32)
    scores = jnp.where(mask[None, :, None, None], scores, MASK_VALUE)
    sink_l = sink.astype(jnp.float32).reshape(N_KV_HEADS, GQA_GROUP)[None, None, :, :, None, None]
    m = jnp.maximum(jnp.max(scores, axis=-1, keepdims=True), sink_l)
    p = jnp.exp(scores - m)
    denom = jnp.sum(p, axis=-1, keepdims=True) + jnp.exp(sink_l - m)
    probs = (p / denom).astype(v.dtype)
    out = jnp.einsum('bnkgqs,bnskd->bnqkgd', probs, vb)
    return out.reshape(b, s, N_HEADS * HEAD_DIM)


def multiscale_pool(p, w_pool, pool_scale):
    b, s, _ = p.shape
    pf = p.astype(jnp.float32)
    cs = jnp.concatenate([jnp.zeros((b, 1, POOL_WIDTH), jnp.float32), jnp.cumsum(pf, axis=1)], axis=1)
    t = jnp.arange(s)
    outs = []
    for g, w in enumerate(POOL_SIZES):
        lo = jnp.clip(t - w // 2, 0, s)
        hi = jnp.clip(t + w // 2, 0, s)
        sl = slice(g * POOL_GROUP_DIM, (g + 1) * POOL_GROUP_DIM)
        csg = cs[:, :, sl]
        mean = (csg[:, hi] - csg[:, lo]) / (hi - lo).astype(jnp.float32)[None, :, None]
        outs.append(mean - pf[:, :, sl])
    d = jnp.stack(outs, axis=2).astype(p.dtype)
    y = jnp.einsum('bsgc,gcd->bsgd', d, w_pool).reshape(b, s, POOL_WIDTH)
    return y * pool_scale


def conv_glu_ffn(h, w_up, conv_w, conv_b, w_down):
    u = h @ w_up
    up = jnp.pad(u, ((0, 0), (1, 1), (0, 0)))
    c = conv_w[0] * up[:, :-2] + conv_w[1] * up[:, 1:-1] + conv_w[2] * up[:, 2:] + conv_b
    val, gate = jnp.split(c, 2, axis=-1)
    return (jax.nn.gelu(gate) * val) @ w_down


def setup_inputs(seed: int = 0) -> dict:
    key = jax.random.key(seed)
    ks = jax.random.split(key, 16)
    f32 = jnp.float32
    nrm = lambda k, shape, scale: jax.random.normal(k, shape, f32) * scale
    return {
        "x": nrm(ks[0], (BATCH, SEQ, D_MODEL), 1.0),
        "w_in": nrm(ks[1], (DEPTH, D_MODEL, IN_COLS), D_MODEL ** -0.5),
        "sink": nrm(ks[2], (DEPTH, N_HEADS), 0.5),
        "w_pool": nrm(ks[3], (DEPTH, N_POOL_GROUPS, POOL_GROUP_DIM, POOL_GROUP_DIM), POOL_GROUP_DIM ** -0.5),
        "pool_scale": 1.0 + nrm(ks[4], (DEPTH, POOL_WIDTH), 0.02),
        "w_out": nrm(ks[5], (DEPTH, D_MIX, D_MODEL), DEEPNORM_BETA * D_MIX ** -0.5),
        "ln1_g": 1.0 + nrm(ks[6], (DEPTH, D_MODEL), 0.02),
        "ln1_b": nrm(ks[7], (DEPTH, D_MODEL), 0.02),
        "w_up": nrm(ks[8], (DEPTH, D_MODEL, 2 * D_FF), D_MODEL ** -0.5),
        "conv_w": nrm(ks[9], (DEPTH, CONV_WIDTH, 2 * D_FF), CONV_WIDTH ** -0.5),
        "conv_b": nrm(ks[10], (DEPTH, 2 * D_FF), 0.01),
        "w_down": nrm(ks[11], (DEPTH, D_FF, D_MODEL), DEEPNORM_BETA * D_FF ** -0.5),
        "ln2_g": 1.0 + nrm(ks[12], (DEPTH, D_MODEL), 0.02),
        "ln2_b": nrm(ks[13], (DEPTH, D_MODEL), 0.02),
        "rel_bias": nrm(ks[14], (N_BUCKETS, N_HEADS), 0.5),
    }


def reference(x, w_in, sink, w_pool, pool_scale, w_out, ln1_g, ln1_b,
              w_up, conv_w, conv_b, w_down, ln2_g, ln2_b, rel_bias):
    b, s, _ = x.shape
    pos_bias, mask = banded_bias_and_mask(rel_bias, s)
    for l in range(DEPTH):
        proj = x @ w_in[l]
        q = proj[..., :Q_COLS].reshape(b, s, N_HEADS, HEAD_DIM)
        k = proj[..., Q_COLS:Q_COLS + KV_COLS].reshape(b, s, N_KV_HEADS, HEAD_DIM)
        v = proj[..., Q_COLS + KV_COLS:Q_COLS + 2 * KV_COLS].reshape(b, s, N_KV_HEADS, HEAD_DIM)
        p = proj[..., Q_COLS + 2 * KV_COLS:]
        attn = windowed_gqa(q, k, v, sink[l], pos_bias, mask)
        pool = multiscale_pool(p, w_pool[l], pool_scale[l])
        mix = jnp.concatenate([attn, pool], axis=-1) @ w_out[l]
        x = layer_norm(DEEPNORM_ALPHA * x + mix, ln1_g[l], ln1_b[l])
        ffn = conv_glu_ffn(x, w_up[l], conv_w[l], conv_b[l], w_down[l])
        x = layer_norm(DEEPNORM_ALPHA * x + ffn, ln2_g[l], ln2_b[l])
    return x
```

```python
import functools
import math

import jax
import jax.numpy as jnp
from jax import lax
from jax.experimental import pallas as pl
from jax.experimental.pallas import tpu as pltpu

F32 = jnp.float32
BF16 = jnp.bfloat16

HEAD_DIM = 64
N_HEADS = 16
N_KV_HEADS = 4
GQA_GROUP = N_HEADS // N_KV_HEADS
Q_COLS = N_HEADS * HEAD_DIM
KV_COLS = N_KV_HEADS * HEAD_DIM
WINDOW = 128
BLOCK = 128
POOL_SIZES = (2, 4, 8, 16)
POOL_GROUP_DIM = 256
POOL_WIDTH = POOL_GROUP_DIM * len(POOL_SIZES)
POOL_HALO = 8
N_BUCKETS = 32
MAX_DISTANCE = 128
LN_EPS = 1e-5
MASK_VALUE = -1e30

V7X_VMEM_LIMIT_BYTES = 56 * 1024 * 1024
MXU_COLS = 256
FFN_HALO = 16

PROJ_ROWS = 512
ATTN_ROWS = 512
POOL_ROWS = 512
OUT_ROWS = 512
FFN_ROWS = 512
FFN_COLS = 512


def _params(*sem):
    return pltpu.CompilerParams(dimension_semantics=sem, vmem_limit_bytes=V7X_VMEM_LIMIT_BYTES)


def _const_spec(shape):
    nd = len(shape)
    return pl.BlockSpec(shape, lambda *_: (0,) * nd)


def _layer_norm(y, g, b):
    mu = jnp.mean(y, axis=-1, keepdims=True)
    yc = y - mu
    var = jnp.mean(yc * yc, axis=-1, keepdims=True)
    return yc * lax.rsqrt(var + LN_EPS) * g + b


def _in_proj_kernel(x_ref, wq_ref, wkv_ref, wp_ref, q_ref, kv_ref, p_ref):
    xb = x_ref[0].astype(BF16)
    nt = (((1,), (1,)), ((), ()))
    q = lax.dot_general(wq_ref[...], xb, nt, preferred_element_type=F32)
    q_ref[0] = (q * (HEAD_DIM ** -0.5)).astype(BF16)
    kv = lax.dot_general(wkv_ref[...], xb, nt, preferred_element_type=F32)
    kv_ref[0] = kv.astype(BF16)
    p_ref[0] = jnp.dot(xb, wp_ref[...], preferred_element_type=F32)


def _in_proj(x, wq_t, wkv_t, wp):
    b, s, d = x.shape
    tm = PROJ_ROWS
    return pl.pallas_call(
        _in_proj_kernel,
        grid=(b, s // tm),
        in_specs=[
            pl.BlockSpec((1, tm, d), lambda bi, i: (bi, i, 0)),
            _const_spec(wq_t.shape),
            _const_spec(wkv_t.shape),
            _const_spec(wp.shape),
        ],
        out_specs=[
            pl.BlockSpec((1, Q_COLS, tm), lambda bi, i: (bi, 0, i)),
            pl.BlockSpec((1, 2 * KV_COLS, tm), lambda bi, i: (bi, 0, i)),
            pl.BlockSpec((1, tm, POOL_WIDTH), lambda bi, i: (bi, i, 0)),
        ],
        out_shape=[
            jax.ShapeDtypeStruct((b, Q_COLS, s), BF16),
            jax.ShapeDtypeStruct((b, 2 * KV_COLS, s), BF16),
            jax.ShapeDtypeStruct((b, s, POOL_WIDTH), F32),
        ],
        compiler_params=_params("parallel", "parallel"),
        name="in_proj",
    )(x, wq_t, wkv_t, wp)


def _attn_kernel(q_ref, kvm_ref, kvp_ref, kvn_ref, bias_ref, sink_ref, o_ref, *, seq):
    i = pl.program_id(1)
    tq = q_ref.shape[2]
    kv = jnp.concatenate([kvp_ref[0], kvm_ref[0], kvn_ref[0]], axis=1)
    row = lax.broadcasted_iota(jnp.int32, (3 * BLOCK, GQA_GROUP * BLOCK), 0)
    col = lax.broadcasted_iota(jnp.int32, (3 * BLOCK, GQA_GROUP * BLOCK), 1) & (BLOCK - 1)
    rel = row - BLOCK - col
    band = (rel <= WINDOW) & (rel >= -WINDOW)
    ks, vts = [], []
    for g in range(N_KV_HEADS):
        kt = kv[g * HEAD_DIM:(g + 1) * HEAD_DIM, :]
        ks.append(kt.astype(F32).T.astype(BF16))
        vts.append(kv[KV_COLS + g * HEAD_DIM:KV_COLS + (g + 1) * HEAD_DIM, :])
    for n in range(tq // BLOCK):
        key_pos = row + (i * tq + (n - 1) * BLOCK)
        valid = band & (key_pos >= 0) & (key_pos < seq)
        for g in range(N_KV_HEADS):
            qg = jnp.concatenate(
                [q_ref[0, (g * GQA_GROUP + h) * HEAD_DIM:(g * GQA_GROUP + h + 1) * HEAD_DIM,
                       n * BLOCK:(n + 1) * BLOCK] for h in range(GQA_GROUP)], axis=1)
            s = jnp.dot(ks[g][n * BLOCK:(n + 3) * BLOCK], qg, preferred_element_type=F32)
            s = jnp.where(valid, s + bias_ref[g], MASK_VALUE)
            sink = sink_ref[g]
            m = jnp.maximum(jnp.max(s, axis=0, keepdims=True), sink)
            p = jnp.exp(s - m)
            denom = jnp.sum(p, axis=0, keepdims=True) + jnp.exp(sink - m)
            o = jnp.dot(vts[g][:, n * BLOCK:(n + 3) * BLOCK], p.astype(BF16),
                        preferred_element_type=F32)
            o = o * (1.0 / denom)
            for hp in range(GQA_GROUP // 2):
                pair = jnp.concatenate([o[:, (2 * hp) * BLOCK:(2 * hp + 1) * BLOCK],
                                        o[:, (2 * hp + 1) * BLOCK:(2 * hp + 2) * BLOCK]], axis=0)
                c0 = (g * GQA_GROUP + 2 * hp) * HEAD_DIM
                o_ref[0, n * BLOCK:(n + 1) * BLOCK, c0:c0 + 2 * HEAD_DIM] = pair.T.astype(BF16)


def _attention(q_t, kv_t, bias_t, sink_t):
    b, _, s = q_t.shape
    tq = ATTN_ROWS
    nb = s // BLOCK
    r = tq // BLOCK
    return pl.pallas_call(
        functools.partial(_attn_kernel, seq=s),
        grid=(b, s // tq),
        in_specs=[
            pl.BlockSpec((1, Q_COLS, tq), lambda bi, i: (bi, 0, i)),
            pl.BlockSpec((1, 2 * KV_COLS, tq), lambda bi, i: (bi, 0, i)),
            pl.BlockSpec((1, 2 * KV_COLS, BLOCK), lambda bi, i: (bi, 0, jnp.maximum(i * r - 1, 0))),
            pl.BlockSpec((1, 2 * KV_COLS, BLOCK), lambda bi, i: (bi, 0, jnp.minimum((i + 1) * r, nb - 1))),
            _const_spec(bias_t.shape),
            _const_spec(sink_t.shape),
        ],
        out_specs=pl.BlockSpec((1, tq, Q_COLS), lambda bi, i: (bi, i, 0)),
        out_shape=jax.ShapeDtypeStruct((b, s, Q_COLS), BF16),
        compiler_params=_params("parallel", "parallel"),
        name="attention",
    )(q_t, kv_t, kv_t, kv_t, bias_t, sink_t)


def _t5_bucket(rel):
    half = N_BUCKETS // 2
    max_exact = half // 2
    base = jnp.where(rel > 0, half, 0)
    n = jnp.abs(rel)
    nf = jnp.maximum(n, 1).astype(F32)
    large = max_exact + (jnp.log(nf / max_exact) / math.log(MAX_DISTANCE / max_exact)
                         * (half - max_exact)).astype(jnp.int32)
    large = jnp.minimum(large, half - 1)
    return base + jnp.where(n < max_exact, n, large)


def _bias_table(rel_bias):
    k_off = jnp.arange(3 * BLOCK)[:, None] - BLOCK
    q_off = jnp.arange(BLOCK)[None, :]
    bias = rel_bias.astype(F32)[_t5_bucket(k_off - q_off)]
    bias = bias.reshape(3 * BLOCK, BLOCK, N_KV_HEADS, GQA_GROUP)
    return jnp.transpose(bias, (2, 0, 3, 1)).reshape(N_KV_HEADS, 3 * BLOCK, GQA_GROUP * BLOCK)


def _pool_kernel(pm_ref, pp_ref, pn_ref, w_ref, sc_ref, o_ref, buf_ref, *, seq):
    i = pl.program_id(1)
    tp = pm_ref.shape[1]
    h = POOL_HALO
    buf_ref[0:h] = jnp.where(i > 0, pp_ref[0], 0.0)
    buf_ref[h:h + tp] = pm_ref[0]
    buf_ref[h + tp:2 * h + tp] = jnp.where(i < pl.num_programs(1) - 1, pn_ref[0], 0.0)
    t = i * tp + lax.broadcasted_iota(jnp.int32, (tp, 1), 0)
    for g, w in enumerate(POOL_SIZES):
        c0 = g * POOL_GROUP_DIM
        acc = buf_ref[h - w // 2:h - w // 2 + tp, c0:c0 + POOL_GROUP_DIM]
        for off in range(-w // 2 + 1, w // 2):
            acc = acc + buf_ref[h + off:h + off + tp, c0:c0 + POOL_GROUP_DIM]
        cnt = jnp.minimum(t + w // 2, seq) - jnp.maximum(t - w // 2, 0)
        d = acc / cnt.astype(F32) - buf_ref[h:h + tp, c0:c0 + POOL_GROUP_DIM]
        y = jnp.dot(d.astype(BF16), w_ref[g], preferred_element_type=F32)
        o_ref[0, :, c0:c0 + POOL_GROUP_DIM] = (y * sc_ref[:, c0:c0 + POOL_GROUP_DIM]).astype(BF16)


def _pool(p, w_pool, pool_scale):
    b, s, c = p.shape
    tp = POOL_ROWS
    r = tp // POOL_HALO
    nh = s // POOL_HALO
    return pl.pallas_call(
        functools.partial(_pool_kernel, seq=s),
        grid=(b, s // tp),
        in_specs=[
            pl.BlockSpec((1, tp, c), lambda bi, i: (bi, i, 0)),
            pl.BlockSpec((1, POOL_HALO, c), lambda bi, i: (bi, jnp.maximum(i * r - 1, 0), 0)),
            pl.BlockSpec((1, POOL_HALO, c), lambda bi, i: (bi, jnp.minimum((i + 1) * r, nh - 1), 0)),
            _const_spec(w_pool.shape),
            _const_spec(pool_scale.shape),
        ],
        out_specs=pl.BlockSpec((1, tp, c), lambda bi, i: (bi, i, 0)),
        out_shape=jax.ShapeDtypeStruct((b, s, c), BF16),
        scratch_shapes=[pltpu.VMEM((tp + 2 * POOL_HALO, c), F32)],
        compiler_params=_params("parallel", "parallel"),
        name="pool",
    )(p, p, p, w_pool, pool_scale)


def _out_proj_kernel(a_ref, p_ref, x_ref, wa_ref, wp_ref, g_ref, b_ref, o_ref, *, alpha):
    mix = jnp.dot(a_ref[...], wa_ref[...], preferred_element_type=F32)
    mix = mix + jnp.dot(p_ref[...], wp_ref[...], preferred_element_type=F32)
    o_ref[...] = _layer_norm(alpha * x_ref[...] + mix, g_ref[...], b_ref[...])


def _out_proj(attn, pool, x, wa, wp, g, bta, alpha):
    n, d = x.shape
    tm = OUT_ROWS
    return pl.pallas_call(
        functools.partial(_out_proj_kernel, alpha=alpha),
        grid=(n // tm,),
        in_specs=[
            pl.BlockSpec((tm, attn.shape[1]), lambda i: (i, 0)),
            pl.BlockSpec((tm, pool.shape[1]), lambda i: (i, 0)),
            pl.BlockSpec((tm, d), lambda i: (i, 0)),
            _const_spec(wa.shape),
            _const_spec(wp.shape),
            _const_spec(g.shape),
            _const_spec(bta.shape),
        ],
        out_specs=pl.BlockSpec((tm, d), lambda i: (i, 0)),
        out_shape=jax.ShapeDtypeStruct((n, d), F32),
        compiler_params=_params("parallel"),
        name="out_proj",
    )(attn, pool, x, wa, wp, g, bta)


def _ffn_kernel(xm_ref, xp_ref, xn_ref, wv_ref, wg_ref, cv_ref, cg_ref, wd_ref, g_ref, b_ref,
                o_ref, lhs_ref, acc_ref, *, alpha, tiles_per_seq):
    i = pl.program_id(0)
    j = pl.program_id(1)
    tm = xm_ref.shape[0]
    h = FFN_HALO

    @pl.when(j == 0)
    def _():
        pos = i % tiles_per_seq
        lhs_ref[0:h] = jnp.where(pos > 0, xp_ref[...], 0.0).astype(BF16)
        lhs_ref[h:h + tm] = xm_ref[...].astype(BF16)
        lhs_ref[h + tm:2 * h + tm] = jnp.where(pos < tiles_per_seq - 1, xn_ref[...], 0.0).astype(BF16)
        acc_ref[...] = jnp.zeros_like(acc_ref)

    lhs = lhs_ref[...]

    def conv(u, c_ref):
        return (c_ref[0:1] * u[h - 1:h - 1 + tm] + c_ref[1:2] * u[h:h + tm]
                + c_ref[2:3] * u[h + 1:h + 1 + tm] + c_ref[3:4])

    val = conv(jnp.dot(lhs, wv_ref[...], preferred_element_type=F32), cv_ref)
    gate = conv(jnp.dot(lhs, wg_ref[...], preferred_element_type=F32), cg_ref)
    act = (jax.nn.gelu(gate) * val).astype(BF16)
    acc_ref[...] += jnp.dot(act, wd_ref[...], preferred_element_type=F32)

    @pl.when(j == pl.num_programs(1) - 1)
    def _():
        o_ref[...] = _layer_norm(alpha * xm_ref[...] + acc_ref[...], g_ref[...], b_ref[...])


def _ffn(x, w_up, conv, w_down, g, bta, alpha, seq):
    n, d = x.shape
    f = w_down.shape[0]
    tm, tf = FFN_ROWS, FFN_COLS
    nf = f // tf
    r = tm // FFN_HALO
    nh = n // FFN_HALO
    return pl.pallas_call(
        functools.partial(_ffn_kernel, alpha=alpha, tiles_per_seq=seq // tm),
        grid=(n // tm, nf),
        in_specs=[
            pl.BlockSpec((tm, d), lambda i, j: (i, 0)),
            pl.BlockSpec((FFN_HALO, d), lambda i, j: (jnp.maximum(i * r - 1, 0), 0)),
            pl.BlockSpec((FFN_HALO, d), lambda i, j: (jnp.minimum((i + 1) * r, nh - 1), 0)),
            pl.BlockSpec((d, tf), lambda i, j: (0, j)),
            pl.BlockSpec((d, tf), lambda i, j: (0, j + nf)),
            pl.BlockSpec((8, tf), lambda i, j: (0, j)),
            pl.BlockSpec((8, tf), lambda i, j: (0, j + nf)),
            pl.BlockSpec((tf, d), lambda i, j: (j, 0)),
            _const_spec(g.shape),
            _const_spec(bta.shape),
        ],
        out_specs=pl.BlockSpec((tm, d), lambda i, j: (i, 0)),
        out_shape=jax.ShapeDtypeStruct((n, d), F32),
        scratch_shapes=[pltpu.VMEM((tm + 2 * FFN_HALO, d), BF16), pltpu.VMEM((tm, d), F32)],
        compiler_params=_params("parallel", "arbitrary"),
        name="ffn",
    )(x, x, x, w_up, w_up, conv, conv, w_down, g, bta)


def _prep_ffn_weights(w_up, conv_w, conv_b, w_down):
    d, two_f = w_up.shape
    f = two_f // 2
    f_pad = -(-f // FFN_COLS) * FFN_COLS
    pad = f_pad - f
    up = jnp.pad(w_up.reshape(d, 2, f), ((0, 0), (0, 0), (0, pad))).reshape(d, 2 * f_pad).astype(BF16)
    taps = jnp.concatenate([conv_w, conv_b[None, :]], axis=0).reshape(4, 2, f)
    taps = jnp.pad(taps, ((0, 4), (0, 0), (0, pad))).reshape(8, 2 * f_pad).astype(F32)
    down = jnp.pad(w_down, ((0, pad), (0, 0))).astype(BF16)
    return up, taps, down


def kernel(x, w_in, sink, w_pool, pool_scale, w_out, ln1_g, ln1_b, w_up, conv_w, conv_b, w_down,
           ln2_g, ln2_b, rel_bias):
    b, s, d = x.shape
    depth = w_in.shape[0]
    alpha = (2 * depth) ** 0.25
    bias_t = _bias_table(rel_bias)
    x = x.astype(F32)
    for l in range(depth):
        wq_t = w_in[l][:, :Q_COLS].T.astype(BF16)
        wkv_t = w_in[l][:, Q_COLS:Q_COLS + 2 * KV_COLS].T.astype(BF16)
        wp = w_in[l][:, Q_COLS + 2 * KV_COLS:].astype(BF16)
        sink_t = jnp.broadcast_to(sink[l].astype(F32).reshape(N_KV_HEADS, 1, GQA_GROUP, 1),
                                  (N_KV_HEADS, 1, GQA_GROUP, BLOCK)).reshape(N_KV_HEADS, 1, GQA_GROUP * BLOCK)
        q_t, kv_t, p = _in_proj(x, wq_t, wkv_t, wp)
        attn = _attention(q_t, kv_t, bias_t, sink_t)
        pool = _pool(p, w_pool[l].astype(BF16), pool_scale[l].astype(F32).reshape(1, POOL_WIDTH))
        wo = w_out[l].astype(BF16)
        x2 = _out_proj(attn.reshape(b * s, Q_COLS), pool.reshape(b * s, POOL_WIDTH), x.reshape(b * s, d),
                       wo[:Q_COLS], wo[Q_COLS:], ln1_g[l].reshape(1, d), ln1_b[l].reshape(1, d), alpha)
        up, taps, down = _prep_ffn_weights(w_up[l], conv_w[l], conv_b[l], w_down[l])
        x2 = _ffn(x2, up, taps, down, ln2_g[l].reshape(1, d), ln2_b[l].reshape(1, d), alpha, s)
        x = x2.reshape(b, s, d)
    return x
```
